```python
import math
import jax, jax.numpy as jnp
from jax import lax
import numpy as np

D_MODEL = 1024
BATCH = 8
SEQ = 4096
DEPTH = 4

A_WIDTH = D_MODEL // 4
A_GROUPS = 4
A_KERNEL = 31
B_HEADS = 4
B_HEAD_DIM = D_MODEL // 16
B_WIDTH = B_HEADS * 2 * B_HEAD_DIM
C_WIDTH = D_MODEL // 4
C_HEADS = 4
C_HEAD_DIM = C_WIDTH // C_HEADS
CHUNK = 128
MIX_WIDTH = A_WIDTH + B_WIDTH + C_WIDTH
IN_WIDTH = 2 * A_WIDTH + 3 * B_WIDTH + 2 * C_WIDTH
FFN_DIM = ((8 * D_MODEL // 3 + 127) // 128) * 128
FFN_KERNEL = 3
ROPE_THETA = 10000.0
ATTN_BLOCK = 128
LN_EPS = 1e-5
DEEPNORM_ALPHA = (2 * DEPTH) ** 0.25
DEEPNORM_BETA = (8 * DEPTH) ** -0.25

kernel_name = "hybrid_conv_diffattn_gmlp_deepnorm"


def layer_norm(x, g, b):
    xf = x.astype(jnp.float32)
    mu = jnp.mean(xf, axis=-1, keepdims=True)
    var = jnp.mean(jnp.square(xf - mu), axis=-1, keepdims=True)
    return ((xf - mu) * lax.rsqrt(var + LN_EPS) * g + b).astype(x.dtype)


def group_norm_channels(x, g, b, groups):
    shp = x.shape
    xf = x.astype(jnp.float32).reshape(shp[:-1] + (groups, shp[-1] // groups))
    mu = jnp.mean(xf, axis=-1, keepdims=True)
    var = jnp.mean(jnp.square(xf - mu), axis=-1, keepdims=True)
    y = ((xf - mu) * lax.rsqrt(var + LN_EPS)).reshape(shp)
    return (y * g + b).astype(x.dtype)


def rms_norm(x, g):
    xf = x.astype(jnp.float32)
    y = xf * lax.rsqrt(jnp.mean(jnp.square(xf), axis=-1, keepdims=True) + LN_EPS)
    return (y * g).astype(x.dtype)


def causal_dwconv(x, w, b):
    k = w.shape[0]
    y = lax.conv_general_dilated(
        x, w[:, None, :].astype(x.dtype), window_strides=(1,), padding=[(k - 1, 0)],
        dimension_numbers=("NWC", "WIO", "NWC"), feature_group_count=x.shape[-1])
    return y + b


def rope_tables(positions):
    inv_freq = 1.0 / (ROPE_THETA ** (jnp.arange(0, B_HEAD_DIM, 2, dtype=jnp.float32) / B_HEAD_DIM))
    ang = positions.astype(jnp.float32)[..., None] * inv_freq
    ang = jnp.concatenate([ang, ang], axis=-1)
    return jnp.cos(ang), jnp.sin(ang)


def apply_rope(x, cos, sin):
    half = x.shape[-1] // 2
    rot = jnp.concatenate([-x[..., half:], x[..., :half]], axis=-1)
    return (x.astype(jnp.float32) * cos + rot.astype(jnp.float32) * sin).astype(x.dtype)


def conformer_conv(z, conv_w, conv_b, gn_g, gn_b):
    a, g = jnp.split(z, 2, axis=-1)
    y = a * jax.nn.sigmoid(g)
    y = causal_dwconv(y, conv_w, conv_b)
    y = group_norm_channels(y, gn_g, gn_b, A_GROUPS)
    return jax.nn.silu(y)


def diff_attention(q, k, v, lam_p, subln_g, cos, sin, lam_init):
    bsz, seq = q.shape[0], q.shape[1]
    q = q.reshape(bsz, seq, B_HEADS, 2, B_HEAD_DIM)
    k = k.reshape(bsz, seq, B_HEADS, 2, B_HEAD_DIM)
    v = v.reshape(bsz, seq, B_HEADS, 2 * B_HEAD_DIM)
    cb, sb = cos[:, :, None, None, :], sin[:, :, None, None, :]
    q = apply_rope(q, cb, sb).transpose(0, 2, 3, 1, 4)
    k = apply_rope(k, cb, sb).transpose(0, 2, 3, 1, 4)
    v = v.transpose(0, 2, 1, 3)
    lp = lam_p.astype(jnp.float32)
    lam = jnp.exp(jnp.sum(lp[0] * lp[1])) - jnp.exp(jnp.sum(lp[2] * lp[3])) + lam_init
    scale = B_HEAD_DIM ** -0.5
    outs = []
    for i in range(seq // ATTN_BLOCK):
        q0 = i * ATTN_BLOCK
        end = q0 + ATTN_BLOCK
        qb = q[:, :, :, q0:end]
        kb = k[:, :, :, :end]
        vb = v[:, :, :end]
        s = jnp.einsum("bhmqd,bhmkd->bhmqk", qb, kb).astype(jnp.float32) * scale
        mask = jnp.arange(end)[None, :] <= (q0 + jnp.arange(ATTN_BLOCK))[:, None]
        p = jax.nn.softmax(jnp.where(mask, s, -jnp.inf), axis=-1)
        a = p[:, :, 0] - lam * p[:, :, 1]
        outs.append(jnp.einsum("bhqk,bhke->bhqe", a.astype(vb.dtype), vb))
    o = jnp.concatenate(outs, axis=2)
    o = (rms_norm(o, subln_g) * (1.0 - lam_init)).astype(v.dtype)
    return o.transpose(0, 2, 1, 3).reshape(bsz, seq, B_WIDTH)


def chunked_spatial_gating(z, ln_g, ln_b, w_sp, b_sp):
    bsz, seq = z.shape[0], z.shape[1]
    z = jax.nn.gelu(z, approximate=False)
    u, v = jnp.split(z, 2, axis=-1)
    v = layer_norm(v, ln_g, ln_b)
    v = v.reshape(bsz, seq // CHUNK, CHUNK, C_HEADS, C_HEAD_DIM)
    w = jnp.where(jnp.tril(jnp.ones((CHUNK, CHUNK), dtype=bool)), w_sp, 0)
    sv = jnp.einsum("hts,bnshd->bnthd", w, v) + b_sp.T[None, None, :, :, None]
    return u * sv.reshape(bsz, seq, C_WIDTH)


def conv_ffn(h, w_up, b_up, conv_w, conv_b, w_down, b_down):
    up = causal_dwconv(h @ w_up + b_up, conv_w, conv_b)
    g, val = jnp.split(up, 2, axis=-1)
    return (jax.nn.silu(g) * val) @ w_down + b_down


def setup_inputs(seed: int = 0) -> dict:
    key = jax.random.key(seed)
    ks = jax.random.split(key, 32)
    f32 = jnp.float32
    nrm = lambda k, shp, s: jax.random.normal(k, shp, f32) * s
    L, D, F = DEPTH, D_MODEL, FFN_DIM
    offset = jax.random.randint(ks[2], (BATCH, 1), 0, 1024, dtype=jnp.int32)
    return {
        "x": nrm(ks[0], (BATCH, SEQ, D), 1.0),
        "c": nrm(ks[1], (BATCH, D), 1.0),
        "positions": offset + jnp.arange(SEQ, dtype=jnp.int32)[None, :],
        "w_ada": nrm(ks[3], (L, D, 6 * D), 0.1 * D ** -0.5),
        "b_ada": nrm(ks[4], (L, 6 * D), 0.01),
        "w_in": nrm(ks[5], (L, D, IN_WIDTH), D ** -0.5),
        "b_in": nrm(ks[6], (L, IN_WIDTH), 0.02),
        "conv_a_w": nrm(ks[7], (L, A_KERNEL, A_WIDTH), A_KERNEL ** -0.5),
        "conv_a_b": nrm(ks[8], (L, A_WIDTH), 0.02),
        "gn_a_g": 1.0 + nrm(ks[9], (L, A_WIDTH), 0.02),
        "gn_a_b": nrm(ks[10], (L, A_WIDTH), 0.02),
        "lam_p": nrm(ks[11], (L, 4, B_HEAD_DIM), 0.1),
        "subln_g": 1.0 + nrm(ks[12], (L, 2 * B_HEAD_DIM), 0.02),
        "ln_c_g": 1.0 + nrm(ks[13], (L, C_WIDTH), 0.02),
        "ln_c_b": nrm(ks[14], (L, C_WIDTH), 0.02),
        "w_sp": nrm(ks[15], (L, C_HEADS, CHUNK, CHUNK), 0.5 * CHUNK ** -0.5),
        "b_sp": 1.0 + nrm(ks[16], (L, C_HEADS, CHUNK), 0.02),
        "w_out": nrm(ks[17], (L, MIX_WIDTH, D), DEEPNORM_BETA * MIX_WIDTH ** -0.5),
        "b_out": nrm(ks[18], (L, D), 0.02),
        "w_up": nrm(ks[19], (L, D, 2 * F), D ** -0.5),
        "b_up": nrm(ks[20], (L, 2 * F), 0.02),
        "conv_f_w": nrm(ks[21], (L, FFN_KERNEL, 2 * F), FFN_KERNEL ** -0.5),
        "conv_f_b": nrm(ks[22], (L, 2 * F), 0.02),
        "w_down": nrm(ks[23], (L, F, D), DEEPNORM_BETA * F ** -0.5),
        "b_down": nrm(ks[24], (L, D), 0.02),
        "ln_g": 1.0 + nrm(ks[25], (L, 2, D), 0.02),
        "ln_b": nrm(ks[26], (L, 2, D), 0.02),
    }


def reference(x, c, positions, w_ada, b_ada, w_in, b_in, conv_a_w, conv_a_b, gn_a_g, gn_a_b,
              lam_p, subln_g, ln_c_g, ln_c_b, w_sp, b_sp, w_out, b_out, w_up, b_up,
              conv_f_w, conv_f_b, w_down, b_down, ln_g, ln_b):
    cos, sin = rope_tables(positions)
    c_act = jax.nn.silu(c)
    splits = [2 * A_WIDTH, 2 * A_WIDTH + B_WIDTH, 2 * A_WIDTH + 2 * B_WIDTH,
              2 * A_WIDTH + 3 * B_WIDTH]
    for l in range(DEPTH):
        lam_init = 0.8 - 0.6 * math.exp(-0.3 * l)
        mod = (c_act @ w_ada[l] + b_ada[l])[:, None, :]
        sh1, sc1, g1, sh2, sc2, g2 = jnp.split(mod, 6, axis=-1)
        h = x * (1.0 + sc1) + sh1
        z = h @ w_in[l] + b_in[l]
        z_a, z_q, z_k, z_v, z_c = jnp.split(z, splits, axis=-1)
        y_a = conformer_conv(z_a, conv_a_w[l], conv_a_b[l], gn_a_g[l], gn_a_b[l])
        y_b = diff_attention(z_q, z_k, z_v, lam_p[l], subln_g[l], cos, sin, lam_init)
        y_c = chunked_spatial_gating(z_c, ln_c_g[l], ln_c_b[l], w_sp[l], b_sp[l])
        y = jnp.concatenate([y_a, y_b, y_c], axis=-1) @ w_out[l] + b_out[l]
        x = layer_norm(DEEPNORM_ALPHA * x + (1.0 + g1) * y, ln_g[l, 0], ln_b[l, 0])
        h = x * (1.0 + sc2) + sh2
        y = conv_ffn(h, w_up[l], b_up[l], conv_f_w[l], conv_f_b[l], w_down[l], b_down[l])
        x = layer_norm(DEEPNORM_ALPHA * x + (1.0 + g2) * y, ln_g[l, 1], ln_b[l, 1])
    return x
```

```python
import functools
import math

import jax
import jax.numpy as jnp
from jax import lax
from jax.experimental import pallas as pl
from jax.experimental.pallas import tpu as pltpu

F32 = jnp.float32
BF16 = jnp.bfloat16

LANES = 128
SUBLANES = 8
VMEM_LIMIT_BYTES = 56 * 1024 * 1024

A_GROUPS = 4
A_KERNEL = 31
B_HEADS = 4
C_HEADS = 4
CHUNK = 128
FFN_KERNEL = 3
ROPE_THETA = 10000.0
LN_EPS = 1e-5
CONV_HALO = 32


def _cparams(*sem):
    return pltpu.CompilerParams(dimension_semantics=sem, vmem_limit_bytes=VMEM_LIMIT_BYTES)


def _layer_norm(r, g, b):
    mu = jnp.mean(r, axis=-1, keepdims=True)
    d = r - mu
    var = jnp.mean(d * d, axis=-1, keepdims=True)
    return d * lax.rsqrt(var + LN_EPS) * g + b


def _mod_kernel(c_ref, w_ref, b_ref, o_ref):
    c = c_ref[...]
    ca = (c * jax.nn.sigmoid(c)).astype(BF16)
    o_ref[...] = jnp.dot(ca, w_ref[...].astype(BF16), preferred_element_type=F32) + b_ref[...]


def _modulation(c, w_ada, b_ada):
    depth, d, n = w_ada.shape
    bsz = c.shape[0]
    tn = 1536 if n % 1536 == 0 else n
    return pl.pallas_call(
        _mod_kernel,
        grid=(depth, n // tn),
        in_specs=[
            pl.BlockSpec((bsz, d), lambda l, j: (0, 0)),
            pl.BlockSpec((None, d, tn), lambda l, j: (l, 0, j)),
            pl.BlockSpec((None, 1, tn), lambda l, j: (l, 0, j)),
        ],
        out_specs=pl.BlockSpec((None, bsz, tn), lambda l, j: (l, 0, j)),
        out_shape=jax.ShapeDtypeStruct((depth, bsz, n), F32),
        compiler_params=_cparams("parallel", "parallel"),
        name="adaln_mod",
    )(c, w_ada, b_ada.reshape(depth, 1, n))


def _rope_kernel(pos_ref, invf_ref, cos_ref, sin_ref):
    ang = pos_ref[...].astype(F32) * invf_ref[...]
    lane = lax.broadcasted_iota(jnp.int32, ang.shape, 1)
    sign = jnp.where((lane % 64) < 32, -1.0, 1.0).astype(F32)
    cos_ref[...] = jnp.cos(ang)
    sin_ref[...] = jnp.sin(ang) * sign


def _rope_tables(positions, head_dim):
    n = positions.size
    inv_freq = 1.0 / (ROPE_THETA ** (jnp.arange(0, head_dim, 2, dtype=F32) / head_dim))
    invf = jnp.tile(inv_freq, LANES // (head_dim // 2)).reshape(1, LANES)
    ts = min(n, 2048)
    out = jax.ShapeDtypeStruct((n, LANES), F32)
    return pl.pallas_call(
        _rope_kernel,
        grid=(n // ts,),
        in_specs=[pl.BlockSpec((ts, 1), lambda i: (i, 0)),
                  pl.BlockSpec((1, LANES), lambda i: (0, 0))],
        out_specs=[pl.BlockSpec((ts, LANES), lambda i: (i, 0))] * 2,
        out_shape=[out, out],
        compiler_params=_cparams("parallel"),
        name="rope_tables",
    )(positions.reshape(n, 1), invf)


def _inproj_kernel(x_ref, mod_ref, cos_ref, sin_ref, w_ref, b_ref, lng_ref, lnb_ref,
                   wsp_ref, bsp_ref, yg_ref, q_ref, k_ref, v_ref, yc_ref, *, a_w, b_w, c_w):
    tm = x_ref.shape[0]
    h = x_ref[...] * (1.0 + mod_ref[1:2, :]) + mod_ref[0:1, :]
    hb = h.astype(BF16)

    def proj(c0, c1):
        return jnp.dot(hb, w_ref[:, c0:c1], preferred_element_type=F32) + b_ref[:, c0:c1]

    za = proj(0, 2 * a_w)
    yg_ref[...] = za[:, :a_w] * jax.nn.sigmoid(za[:, a_w:])

    cos = cos_ref[...]
    sin = sin_ref[...]
    lane = lax.broadcasted_iota(jnp.int32, (tm, LANES), 1)
    first_half = (lane % 64) < 32
    c0 = 2 * a_w
    for dst, scale in ((q_ref, 0.125), (k_ref, None)):
        z = proj(c0, c0 + b_w)
        for hh in range(b_w // LANES):
            zz = z[:, hh * LANES:(hh + 1) * LANES]
            rot = jnp.where(first_half, pltpu.roll(zz, 96, 1), pltpu.roll(zz, 32, 1))
            r = zz * cos + rot * sin
            if scale is not None:
                r = r * scale
            dst[:, hh * LANES:(hh + 1) * LANES] = r.astype(BF16)
        c0 += b_w
    v_ref[...] = proj(c0, c0 + b_w).astype(BF16)
    c0 += b_w

    zc = proj(c0, c0 + 2 * c_w)
    zc = 0.5 * zc * (1.0 + lax.erf(zc * math.sqrt(0.5)))
    u = zc[:, :c_w]
    vn = _layer_norm(zc[:, c_w:], lng_ref[...], lnb_ref[...])
    hd = c_w // C_HEADS
    row = lax.broadcasted_iota(jnp.int32, (CHUNK, C_HEADS * CHUNK), 0)
    col = lax.broadcasted_iota(jnp.int32, (CHUNK, C_HEADS * CHUNK), 1)
    wcat = jnp.where((col % CHUNK) <= row, wsp_ref[...], 0.0).astype(BF16)
    ch_lane = lax.broadcasted_iota(jnp.int32, (CHUNK, c_w), 1) // hd
    for ch in range(tm // CHUNK):
        vch = vn[ch * CHUNK:(ch + 1) * CHUNK, :]
        vstack = jnp.concatenate(
            [jnp.where(ch_lane == hh, vch, 0.0) for hh in range(C_HEADS)], axis=0).astype(BF16)
        sv = jnp.dot(wcat, vstack, preferred_element_type=F32) + bsp_ref[...]
        yc_ref[ch * CHUNK:(ch + 1) * CHUNK, :] = (u[ch * CHUNK:(ch + 1) * CHUNK, :] * sv).astype(BF16)


def _in_projection(l, x2, mod, cos_t, sin_t, w_in, b_in, ln_c_g, ln_c_b, wsp, bsp, a_w, seq, tm):
    n, d = x2.shape
    c_w = ln_c_g.shape[-1]
    in_w = w_in.shape[-1]
    b_w = (in_w - 2 * a_w - 2 * c_w) // 3
    row = lambda i: (i, 0)
    lay = lambda i: (l, 0, 0)
    kern = functools.partial(_inproj_kernel, a_w=a_w, b_w=b_w, c_w=c_w)
    return pl.pallas_call(
        kern,
        grid=(n // tm,),
        in_specs=[
            pl.BlockSpec((tm, d), row),
            pl.BlockSpec((None, None, 6, d), lambda i: (l, (i * tm) // seq, 0, 0)),
            pl.BlockSpec((tm, LANES), row),
            pl.BlockSpec((tm, LANES), row),
            pl.BlockSpec((None, d, in_w), lay),
            pl.BlockSpec((None, 1, in_w), lay),
            pl.BlockSpec((None, 1, c_w), lay),
            pl.BlockSpec((None, 1, c_w), lay),
            pl.BlockSpec((None, CHUNK, C_HEADS * CHUNK), lay),
            pl.BlockSpec((None, CHUNK, c_w), lay),
        ],
        out_specs=[
            pl.BlockSpec((tm, a_w), row),
            pl.BlockSpec((tm, b_w), row),
            pl.BlockSpec((tm, b_w), row),
            pl.BlockSpec((tm, b_w), row),
            pl.BlockSpec((tm, c_w), row),
        ],
        out_shape=[
            jax.ShapeDtypeStruct((n, a_w), F32),
            jax.ShapeDtypeStruct((n, b_w), BF16),
            jax.ShapeDtypeStruct((n, b_w), BF16),
            jax.ShapeDtypeStruct((n, b_w), BF16),
            jax.ShapeDtypeStruct((n, c_w), BF16),
        ],
        compiler_params=_cparams("parallel"),
        name="in_proj",
    )(x2, mod, cos_t, sin_t, w_in, b_in, ln_c_g, ln_c_b, wsp, bsp)


def _conva_kernel(yg_ref, w_ref, cb_ref, gmat_ref, g_ref, b_ref, o_ref, pad_ref):
    ts, cw = yg_ref.shape

    @pl.when(pl.program_id(1) == 0)
    def _():
        pad_ref[0:CONV_HALO, :] = jnp.zeros((CONV_HALO, cw), F32)

    pad_ref[CONV_HALO:CONV_HALO + ts, :] = yg_ref[...]
    base = CONV_HALO - (A_KERNEL - 1)
    acc = jnp.zeros((ts, cw), F32) + cb_ref[...]
    for phase in range(SUBLANES):
        taps = [k for k in range(A_KERNEL) if (base + k) % SUBLANES == phase]
        if not taps:
            continue
        lo = base + taps[0]
        hi = base + taps[-1]
        window = pad_ref[lo:hi + ts, :]
        for k in taps:
            off = base + k - lo
            acc = acc + window[off:off + ts, :] * w_ref[k:k + 1, :]
    pad_ref[0:CONV_HALO, :] = pad_ref[ts:ts + CONV_HALO, :]

    gmat = gmat_ref[...]

    def group_mean(t):
        hi = t.astype(BF16)
        lo = (t - hi.astype(F32)).astype(BF16)
        return (jnp.dot(hi, gmat, preferred_element_type=F32)
                + jnp.dot(lo, gmat, preferred_element_type=F32))

    d = acc - group_mean(acc)
    var = group_mean(d * d)
    y = d * lax.rsqrt(var + LN_EPS) * g_ref[...] + b_ref[...]
    o_ref[...] = (y * jax.nn.sigmoid(y)).astype(BF16)


def _conformer_conv(l, yg3, conv_w, conv_b, gmat, gn_g, gn_b, ts):
    bsz, seq, cw = yg3.shape
    lay = lambda b, s: (l, 0, 0)
    return pl.pallas_call(
        _conva_kernel,
        grid=(bsz, seq // ts),
        in_specs=[
            pl.BlockSpec((None, ts, cw), lambda b, s: (b, s, 0)),
            pl.BlockSpec((None, A_KERNEL, cw), lay),
            pl.BlockSpec((None, 1, cw), lay),
            pl.BlockSpec((cw, cw), lambda b, s: (0, 0)),
            pl.BlockSpec((None, 1, cw), lay),
            pl.BlockSpec((None, 1, cw), lay),
        ],
        out_specs=pl.BlockSpec((None, ts, cw), lambda b, s: (b, s, 0)),
        out_shape=jax.ShapeDtypeStruct((bsz, seq, cw), BF16),
        scratch_shapes=[pltpu.VMEM((ts + CONV_HALO, cw), F32)],
        compiler_params=_cparams("parallel", "arbitrary"),
        name="conformer_conv",
    )(yg3, conv_w, conv_b, gmat, gn_g, gn_b)


def _attn_kernel(q_ref, k_ref, v_ref, lam_ref, g_ref, o_ref, *, tq, tk, lam_init):
    qi = pl.program_id(2)
    qb = q_ref[...]
    lane = lax.broadcasted_iota(jnp.int32, qb.shape, 1)
    zero = jnp.zeros_like(qb)
    q2 = jnp.concatenate([jnp.where(lane < 64, qb, zero), jnp.where(lane >= 64, qb, zero)], axis=0)

    def step(start, carry, masked):
        m, l, acc = carry
        kb = k_ref[pl.ds(start, tk), :]
        vb = v_ref[pl.ds(start, tk), :]
        s = lax.dot_general(q2, kb, (((1,), (1,)), ((), ())), preferred_element_type=F32)
        if masked:
            r = lax.broadcasted_iota(jnp.int32, s.shape, 0)
            c = lax.broadcasted_iota(jnp.int32, s.shape, 1)
            qpos = jnp.where(r >= tq, r - tq, r)
            s = jnp.where(c <= qpos, s, -jnp.inf)
        m_new = jnp.maximum(m, jnp.max(s, axis=-1, keepdims=True))
        alpha = jnp.exp(m - m_new)
        p = jnp.exp(s - m_new)
        l = alpha * l + jnp.sum(p, axis=-1, keepdims=True)
        acc = alpha * acc + jnp.dot(p.astype(BF16), vb, preferred_element_type=F32)
        return m_new, l, acc

    init = (jnp.full((2 * tq, 1), -jnp.inf, F32), jnp.zeros((2 * tq, 1), F32),
            jnp.zeros((2 * tq, LANES), F32))
    n_full = (qi * tq) // tk
    carry = lax.fori_loop(
        0, n_full, lambda j, cr: step(pl.multiple_of(j * tk, tk), cr, False), init)
    m, l, acc = step(pl.multiple_of(qi * tq, tq), carry, True)

    lp = lam_ref[...]
    lam = (jnp.exp(jnp.sum(lp[0:1, :] * lp[1:2, :], axis=-1, keepdims=True))
           - jnp.exp(jnp.sum(lp[2:3, :] * lp[3:4, :], axis=-1, keepdims=True)) + lam_init)
    o = acc[:tq, :] / l[:tq, :] - lam * (acc[tq:, :] / l[tq:, :])
    o = o * lax.rsqrt(jnp.mean(o * o, axis=-1, keepdims=True) + LN_EPS) * g_ref[...]
    o_ref[...] = (o * (1.0 - lam_init)).astype(BF16)


def _diff_attention(l, q3, k3, v3, lam_p, subln_g, lam_init, tq):
    bsz, seq, bw = q3.shape
    heads = bw // LANES
    kern = functools.partial(_attn_kernel, tq=tq, tk=tq, lam_init=lam_init)
    return pl.pallas_call(
        kern,
        grid=(bsz, heads, seq // tq),
        in_specs=[
            pl.BlockSpec((None, tq, LANES), lambda b, h, i: (b, i, h)),
            pl.BlockSpec((None, seq, LANES), lambda b, h, i: (b, 0, h)),
            pl.BlockSpec((None, seq, LANES), lambda b, h, i: (b, 0, h)),
            pl.BlockSpec((None, 4, lam_p.shape[-1]), lambda b, h, i: (l, 0, 0)),
            pl.BlockSpec((None, 1, LANES), lambda b, h, i: (l, 0, 0)),
        ],
        out_specs=pl.BlockSpec((None, tq, LANES), lambda b, h, i: (b, i, h)),
        out_shape=jax.ShapeDtypeStruct((bsz, seq, bw), BF16),
        compiler_params=_cparams("parallel", "parallel", "arbitrary"),
        name="diff_attn",
    )(q3, k3, v3, lam_p, subln_g)


def _outproj_kernel(ya_ref, yb_ref, yc_ref, x_ref, mod_ref, w_ref, b_ref, g_ref, beta_ref, o_ref,
                    *, alpha):
    a_w = ya_ref.shape[1]
    b_w = yb_ref.shape[1]
    y = (jnp.dot(ya_ref[...], w_ref[0:a_w, :], preferred_element_type=F32)
         + jnp.dot(yb_ref[...], w_ref[a_w:a_w + b_w, :], preferred_element_type=F32)
         + jnp.dot(yc_ref[...], w_ref[a_w + b_w:, :], preferred_element_type=F32)
         + b_ref[...])
    r = alpha * x_ref[...] + (1.0 + mod_ref[2:3, :]) * y
    o_ref[...] = _layer_norm(r, g_ref[...], beta_ref[...])


def _out_projection(l, ya, yb, yc, x2, mod, w_out, b_out, ln_g, ln_b, seq, tm, alpha):
    n, d = x2.shape
    row = lambda i: (i, 0)
    kern = functools.partial(_outproj_kernel, alpha=alpha)
    return pl.pallas_call(
        kern,
        grid=(n // tm,),
        in_specs=[
            pl.BlockSpec((tm, ya.shape[1]), row),
            pl.BlockSpec((tm, yb.shape[1]), row),
            pl.BlockSpec((tm, yc.shape[1]), row),
            pl.BlockSpec((tm, d), row),
            pl.BlockSpec((None, None, 6, d), lambda i: (l, (i * tm) // seq, 0, 0)),
            pl.BlockSpec((None,) + w_out.shape[1:], lambda i: (l, 0, 0)),
            pl.BlockSpec((None, 1, d), lambda i: (l, 0, 0)),
            pl.BlockSpec((None, None, 1, d), lambda i: (l, 0, 0, 0)),
            pl.BlockSpec((None, None, 1, d), lambda i: (l, 0, 0, 0)),
        ],
        out_specs=pl.BlockSpec((tm, d), row),
        out_shape=jax.ShapeDtypeStruct((n, d), F32),
        compiler_params=_cparams("parallel"),
        name="out_proj_ln",
    )(ya, yb, yc, x2, mod, w_out, b_out, ln_g, ln_b)


def _ffn_kernel(x_ref, xh_ref, mod_ref, wg_ref, wv_ref, bg_ref, bv_ref, cwg_ref, cwv_ref,
                cbg_ref, cbv_ref, wd_ref, bd_ref, g_ref, beta_ref, o_ref,
                hb_ref, up_ref, hid_ref, *, alpha, seq, n_chunks, tf):
    tm = x_ref.shape[0]
    halo = xh_ref.shape[0]
    sc = 1.0 + mod_ref[4:5, :]
    sh = mod_ref[3:4, :]
    x = x_ref[...]
    hb_ref[halo:, :] = (x * sc + sh).astype(BF16)
    hb_ref[0:halo, :] = (xh_ref[...] * sc + sh).astype(BF16)
    seq_start = (pl.program_id(0) * tm) % seq == 0

    def conv3(z, bias_ref, cw_ref, cb_ref, c):
        up_ref[...] = z + bias_ref[c]

        @pl.when(seq_start)
        def _():
            up_ref[0:halo, :] = jnp.zeros((halo, tf), F32)

        cw = cw_ref[c]
        out = cb_ref[c]
        for k in range(FFN_KERNEL):
            off = halo - (FFN_KERNEL - 1) + k
            out = out + up_ref[off:off + tm, :] * cw[k:k + 1, :]
        return out

    hb = hb_ref[...]
    for c in range(n_chunks):
        gate = conv3(jnp.dot(hb, wg_ref[c], preferred_element_type=F32), bg_ref, cwg_ref, cbg_ref, c)
        gate = gate * jax.nn.sigmoid(gate)
        val = conv3(jnp.dot(hb, wv_ref[c], preferred_element_type=F32), bv_ref, cwv_ref, cbv_ref, c)
        hid_ref[:, c * tf:(c + 1) * tf] = (gate * val).astype(BF16)

    y = jnp.dot(hid_ref[...], wd_ref[...], preferred_element_type=F32) + bd_ref[...]
    r = alpha * x + (1.0 + mod_ref[5:6, :]) * y
    o_ref[...] = _layer_norm(r, g_ref[...], beta_ref[...])


def _conv_ffn(l, x2, mod, wg, wv, bg, bv, cwg, cwv, cbg, cbv, w_down, b_down, ln_g, ln_b,
              seq, tm, alpha):
    n, d = x2.shape
    n_chunks, _, tf = wg.shape[1:]
    f = n_chunks * tf
    halo = SUBLANES
    row = lambda i: (i, 0)
    lay4 = lambda i: (l, 0, 0, 0)
    lay3 = lambda i: (l, 0, 0)
    kern = functools.partial(_ffn_kernel, alpha=alpha, seq=seq, n_chunks=n_chunks, tf=tf)
    return pl.pallas_call(
        kern,
        grid=(n // tm,),
        in_specs=[
            pl.BlockSpec((tm, d), row),
            pl.BlockSpec((halo, d), lambda i: (jnp.maximum(i * (tm // halo) - 1, 0), 0)),
            pl.BlockSpec((None, None, 6, d), lambda i: (l, (i * tm) // seq, 0, 0)),
            pl.BlockSpec((None, n_chunks, d, tf), lay4),
            pl.BlockSpec((None, n_chunks, d, tf), lay4),
            pl.BlockSpec((None, n_chunks, 1, tf), lay4),
            pl.BlockSpec((None, n_chunks, 1, tf), lay4),
            pl.BlockSpec((None, n_chunks, FFN_KERNEL, tf), lay4),
            pl.BlockSpec((None, n_chunks, FFN_KERNEL, tf), lay4),
            pl.BlockSpec((None, n_chunks, 1, tf), lay4),
            pl.BlockSpec((None, n_chunks, 1, tf), lay4),
            pl.BlockSpec((None, f, d), lay3),
            pl.BlockSpec((None, 1, d), lay3),
            pl.BlockSpec((None, None, 1, d), lambda i: (l, 1, 0, 0)),
            pl.BlockSpec((None, None, 1, d), lambda i: (l, 1, 0, 0)),
        ],
        out_specs=pl.BlockSpec((tm, d), row),
        out_shape=jax.ShapeDtypeStruct((n, d), F32),
        scratch_shapes=[
            pltpu.VMEM((tm + halo, d), BF16),
            pltpu.VMEM((tm + halo, tf), F32),
            pltpu.VMEM((tm, f), BF16),
        ],
        compiler_params=_cparams("parallel"),
        name="conv_ffn_ln",
    )(x2, x2, mod, wg, wv, bg, bv, cwg, cwv, cbg, cbv, w_down, b_down, ln_g, ln_b)


def _chunk_cols(w, n_chunks, tf):
    depth, r, _ = w.shape
    return w.reshape(depth, r, n_chunks, tf).transpose(0, 2, 1, 3)


def kernel(x, c, positions, w_ada, b_ada, w_in, b_in, conv_a_w, conv_a_b, gn_a_g, gn_a_b, lam_p, subln_g, ln_c_g, ln_c_b, w_sp, b_sp, w_out, b_out, w_up, b_up, conv_f_w, conv_f_b, w_down, b_down, ln_g, ln_b):
    bsz, seq, d = x.shape
    depth = w_ada.shape[0]
    n = bsz * seq
    a_w = conv_a_w.shape[-1]
    c_w = ln_c_g.shape[-1]
    head_dim = lam_p.shape[-1]
    f = w_down.shape[1]
    alpha = (2 * depth) ** 0.25
    tm = min(512, seq)
    tf = 256
    n_chunks = f // tf

    mod = _modulation(c, w_ada, b_ada).reshape(depth, bsz, 6, d)
    cos_t, sin_t = _rope_tables(positions, head_dim)
    w_in_b = w_in.astype(BF16)
    b_in3 = b_in.reshape(depth, 1, -1)
    wsp = w_sp.transpose(0, 2, 1, 3).reshape(depth, CHUNK, C_HEADS * CHUNK)
    bsp = jnp.repeat(b_sp.transpose(0, 2, 1), c_w // C_HEADS, axis=-1)
    gidx = jnp.arange(a_w) // (a_w // A_GROUPS)
    gmat = jnp.where(gidx[:, None] == gidx[None, :], 1.0 / (a_w // A_GROUPS), 0.0).astype(BF16)
    w_out_b = w_out.astype(BF16)
    w_up_b = w_up.astype(BF16)
    wg = _chunk_cols(w_up_b[:, :, :f], n_chunks, tf)
    wv = _chunk_cols(w_up_b[:, :, f:], n_chunks, tf)
    bg = _chunk_cols(b_up[:, None, :f], n_chunks, tf)
    bv = _chunk_cols(b_up[:, None, f:], n_chunks, tf)
    cwg = _chunk_cols(conv_f_w[:, :, :f], n_chunks, tf)
    cwv = _chunk_cols(conv_f_w[:, :, f:], n_chunks, tf)
    cbg = _chunk_cols(conv_f_b[:, None, :f], n_chunks, tf)
    cbv = _chunk_cols(conv_f_b[:, None, f:], n_chunks, tf)
    w_down_b = w_down.astype(BF16)
    vec3 = lambda a: a.reshape(depth, 1, -1)
    ln_g4 = ln_g.reshape(depth, 2, 1, d)
    ln_b4 = ln_b.reshape(depth, 2, 1, d)

    x2 = x.reshape(n, d)
    for l in range(depth):
        lam_init = 0.8 - 0.6 * math.exp(-0.3 * l)
        yg, q, k, v, yc = _in_projection(l, x2, mod, cos_t, sin_t, w_in_b, b_in3,
                                         vec3(ln_c_g), vec3(ln_c_b), wsp, bsp, a_w, seq, tm)
        ya = _conformer_conv(l, yg.reshape(bsz, seq, a_w), conv_a_w, vec3(conv_a_b), gmat,
                             vec3(gn_a_g), vec3(gn_a_b), tm)
        bw = q.shape[-1]
        yb = _diff_attention(l, q.reshape(bsz, seq, bw), k.reshape(bsz, seq, bw),
                             v.reshape(bsz, seq, bw), lam_p, vec3(subln_g), lam_init, tm)
        x2 = _out_projection(l, ya.reshape(n, a_w), yb.reshape(n, bw), yc, x2, mod, w_out_b,
                             vec3(b_out), ln_g4, ln_b4, seq, tm, alpha)
        x2 = _conv_ffn(l, x2, mod, wg, wv, bg, bv, cwg, cwv, cbg, cbv, w_down_b, vec3(b_down),
                       ln_g4, ln_b4, seq, tm, alpha)
    return x2.reshape(bsz, seq, d)
```

```python
import functools
import math

import jax
import jax.numpy as jnp
from jax import lax
from jax.experimental import pallas as pl
from jax.experimental.pallas import tpu as pltpu

F32 = jnp.float32
BF16 = jnp.bfloat16

LANES = 128
SUBLANES = 8
VMEM_LIMIT_BYTES = 56 * 1024 * 1024

A_GROUPS = 4
A_KERNEL = 31
B_HEADS = 4
C_HEADS = 4
CHUNK = 128
FFN_KERNEL = 3
ROPE_THETA = 10000.0
LN_EPS = 1e-5
CONV_HALO = 32


def _cparams(*sem):
    return pltpu.CompilerParams(dimension_semantics=sem, vmem_limit_bytes=VMEM_LIMIT_BYTES)


def _layer_norm(r, g, b):
    mu = jnp.mean(r, axis=-1, keepdims=True)
    d = r - mu
    var = jnp.mean(d * d, axis=-1, keepdims=True)
    return d * lax.rsqrt(var + LN_EPS) * g + b


def _mod_kernel(c_ref, w_ref, b_ref, o_ref):
    c = c_ref[...]
    ca = (c * jax.nn.sigmoid(c)).astype(BF16)
    o_ref[...] = jnp.dot(ca, w_ref[...].astype(BF16), preferred_element_type=F32) + b_ref[...]


def _modulation(c, w_ada, b_ada):
    depth, d, n = w_ada.shape
    bsz = c.shape[0]
    tn = 1536 if n % 1536 == 0 else n
    return pl.pallas_call(
        _mod_kernel,
        grid=(depth, n // tn),
        in_specs=[
            pl.BlockSpec((bsz, d), lambda l, j: (0, 0)),
            pl.BlockSpec((None, d, tn), lambda l, j: (l, 0, j)),
            pl.BlockSpec((None, 1, tn), lambda l, j: (l, 0, j)),
        ],
        out_specs=pl.BlockSpec((None, bsz, tn), lambda l, j: (l, 0, j)),
        out_shape=jax.ShapeDtypeStruct((depth, bsz, n), F32),
        compiler_params=_cparams("parallel", "parallel"),
        name="adaln_mod",
    )(c, w_ada, b_ada.reshape(depth, 1, n))


def _rope_kernel(pos_ref, invf_ref, cos_ref, sin_ref):
    ang = pos_ref[...].astype(F32) * invf_ref[...]
    lane = lax.broadcasted_iota(jnp.int32, ang.shape, 1)
    sign = jnp.where((lane % 64) < 32, -1.0, 1.0).astype(F32)
    cos_ref[...] = jnp.cos(ang)
    sin_ref[...] = jnp.sin(ang) * sign


def _rope_tables(positions, head_dim):
    n = positions.size
    inv_freq = 1.0 / (ROPE_THETA ** (jnp.arange(0, head_dim, 2, dtype=F32) / head_dim))
    invf = jnp.tile(inv_freq, LANES // (head_dim // 2)).reshape(1, LANES)
    ts = min(n, 2048)
    out = jax.ShapeDtypeStruct((n, LANES), F32)
    return pl.pallas_call(
        _rope_kernel,
        grid=(n // ts,),
        in_specs=[pl.BlockSpec((ts, 1), lambda i: (i, 0)),
                  pl.BlockSpec((1, LANES), lambda i: (0, 0))],
        out_specs=[pl.BlockSpec((ts, LANES), lambda i: (i, 0))] * 2,
        out_shape=[out, out],
        compiler_params=_cparams("parallel"),
        name="rope_tables",
    )(positions.reshape(n, 1), invf)


def _inproj_kernel(x_ref, mod_ref, cos_ref, sin_ref, w_ref, b_ref, lng_ref, lnb_ref,
                   wsp_ref, bsp_ref, yg_ref, q_ref, k_ref, v_ref, yc_ref, *, a_w, b_w, c_w):
    tm = x_ref.shape[0]
    h = x_ref[...] * (1.0 + mod_ref[1:2, :]) + mod_ref[0:1, :]
    hb = h.astype(BF16)

    def proj(c0, c1):
        return jnp.dot(hb, w_ref[:, c0:c1], preferred_element_type=F32) + b_ref[:, c0:c1]

    za = proj(0, 2 * a_w)
    yg_ref[...] = za[:, :a_w] * jax.nn.sigmoid(za[:, a_w:])

    cos = cos_ref[...]
    sin = sin_ref[...]
    lane = lax.broadcasted_iota(jnp.int32, (tm, LANES), 1)
    first_half = (lane % 64) < 32
    c0 = 2 * a_w
    for dst, scale in ((q_ref, 0.125), (k_ref, None)):
        z = proj(c0, c0 + b_w)
        for hh in range(b_w // LANES):
            zz = z[:, hh * LANES:(hh + 1) * LANES]
            rot = jnp.where(first_half, pltpu.roll(zz, 96, 1), pltpu.roll(zz, 32, 1))
            r = zz * cos + rot * sin
            if scale is not None:
                r = r * scale
            dst[:, hh * LANES:(hh + 1) * LANES] = r.astype(BF16)
        c0 += b_w
    v_ref[...] = proj(c0, c0 + b_w).astype(BF16)
    c0 += b_w

    zc = proj(c0, c0 + 2 * c_w)
    zc = 0.5 * zc * (1.0 + lax.erf(zc * math.sqrt(0.5)))
    u = zc[:, :c_w]
    vn = _layer_norm(zc[:, c_w:], lng_ref[...], lnb_ref[...])
    hd = c_w // C_HEADS
    row = lax.broadcasted_iota(jnp.int32, (CHUNK, C_HEADS * CHUNK), 0)
    col = lax.broadcasted_iota(jnp.int32, (CHUNK, C_HEADS * CHUNK), 1)
    wcat = jnp.where((col % CHUNK) <= row, wsp_ref[...], 0.0).astype(BF16)
    ch_lane = lax.broadcasted_iota(jnp.int32, (CHUNK, c_w), 1) // hd
    for ch in range(tm // CHUNK):
        vch = vn[ch * CHUNK:(ch + 1) * CHUNK, :]
        vstack = jnp.concatenate(
            [jnp.where(ch_lane == hh, vch, 0.0) for hh in range(C_HEADS)], axis=0).astype(BF16)
        sv = jnp.dot(wcat, vstack, preferred_element_type=F32) + bsp_ref[...]
        yc_ref[ch * CHUNK:(ch + 1) * CHUNK, :] = (u[ch * CHUNK:(ch + 1) * CHUNK, :] * sv).astype(BF16)


def _in_projection(l, x2, mod, cos_t, sin_t, w_in, b_in, ln_c_g, ln_c_b, wsp, bsp, a_w, seq, tm):
    n, d = x2.shape
    c_w = ln_c_g.shape[-1]
    in_w = w_in.shape[-1]
    b_w = (in_w - 2 * a_w - 2 * c_w) // 3
    row = lambda i: (i, 0)
    lay = lambda i: (l, 0, 0)
    kern = functools.partial(_inproj_kernel, a_w=a_w, b_w=b_w, c_w=c_w)
    return pl.pallas_call(
        kern,
        grid=(n // tm,),
        in_specs=[
            pl.BlockSpec((tm, d), row),
            pl.BlockSpec((None, None, 6, d), lambda i: (l, (i * tm) // seq, 0, 0)),
            pl.BlockSpec((tm, LANES), row),
            pl.BlockSpec((tm, LANES), row),
            pl.BlockSpec((None, d, in_w), lay),
            pl.BlockSpec((None, 1, in_w), lay),
            pl.BlockSpec((None, 1, c_w), lay),
            pl.BlockSpec((None, 1, c_w), lay),
            pl.BlockSpec((None, CHUNK, C_HEADS * CHUNK), lay),
            pl.BlockSpec((None, CHUNK, c_w), lay),
        ],
        out_specs=[
            pl.BlockSpec((tm, a_w), row),
            pl.BlockSpec((tm, b_w), row),
            pl.BlockSpec((tm, b_w), row),
            pl.BlockSpec((tm, b_w), row),
            pl.BlockSpec((tm, c_w), row),
        ],
        out_shape=[
            jax.ShapeDtypeStruct((n, a_w), F32),
            jax.ShapeDtypeStruct((n, b_w), BF16),
            jax.ShapeDtypeStruct((n, b_w), BF16),
            jax.ShapeDtypeStruct((n, b_w), BF16),
            jax.ShapeDtypeStruct((n, c_w), BF16),
        ],
        compiler_params=_cparams("parallel"),
        name="in_proj",
    )(x2, mod, cos_t, sin_t, w_in, b_in, ln_c_g, ln_c_b, wsp, bsp)


def _conva_kernel(yg_ref, w_ref, cb_ref, gmat_ref, g_ref, b_ref, o_ref, pad_ref):
    ts, cw = yg_ref.shape

    @pl.when(pl.program_id(1) == 0)
    def _():
        pad_ref[0:CONV_HALO, :] = jnp.zeros((CONV_HALO, cw), F32)

    pad_ref[CONV_HALO:CONV_HALO + ts, :] = yg_ref[...]
    base = CONV_HALO - (A_KERNEL - 1)
    acc = jnp.zeros((ts, cw), F32) + cb_ref[...]
    for phase in range(SUBLANES):
        taps = [k for k in range(A_KERNEL) if (base + k) % SUBLANES == phase]
        if not taps:
            continue
        lo = base + taps[0]
        hi = base + taps[-1]
        window = pad_ref[lo:hi + ts, :]
        for k in taps:
            off = base + k - lo
            acc = acc + window[off:off + ts, :] * w_ref[k:k + 1, :]
    pad_ref[0:CONV_HALO, :] = pad_ref[ts:ts + CONV_HALO, :]

    gmat = gmat_ref[...]

    def group_mean(t):
        hi = t.astype(BF16)
        lo = (t - hi.astype(F32)).astype(BF16)
        return (jnp.dot(hi, gmat, preferred_element_type=F32)
                + jnp.dot(lo, gmat, preferred_element_type=F32))

    d = acc - group_mean(acc)
    var = group_mean(d * d)
    y = d * lax.rsqrt(var + LN_EPS) * g_ref[...] + b_ref[...]
    o_ref[...] = (y * jax.nn.sigmoid(y)).astype(BF16)


def _conformer_conv(l, yg3, conv_w, conv_b, gmat, gn_g, gn_b, ts):
    bsz, seq, cw = yg3.shape
    lay = lambda b, s: (l, 0, 0)
    return pl.pallas_call(
        _conva_kernel,
        grid=(bsz, seq // ts),
        in_specs=[
            pl.BlockSpec((None, ts, cw), lambda b, s: (b, s, 0)),
            pl.BlockSpec((None, A_KERNEL, cw), lay),
            pl.BlockSpec((None, 1, cw), lay),
            pl.BlockSpec((cw, cw), lambda b, s: (0, 0)),
            pl.BlockSpec((None, 1, cw), lay),
            pl.BlockSpec((None, 1, cw), lay),
        ],
        out_specs=pl.BlockSpec((None, ts, cw), lambda b, s: (b, s, 0)),
        out_shape=jax.ShapeDtypeStruct((bsz, seq, cw), BF16),
        scratch_shapes=[pltpu.VMEM((ts + CONV_HALO, cw), F32)],
        compiler_params=_cparams("parallel", "arbitrary"),
        name="conformer_conv",
    )(yg3, conv_w, conv_b, gmat, gn_g, gn_b)


def _attn_kernel(q_ref, k_ref, v_ref, lam_ref, g_ref, o_ref, *, tq, tk, lam_init):
    qi = pl.program_id(2)
    qb = q_ref[...]
    lane = lax.broadcasted_iota(jnp.int32, qb.shape, 1)
    zero = jnp.zeros_like(qb)
    q2 = jnp.concatenate([jnp.where(lane < 64, qb, zero), jnp.where(lane >= 64, qb, zero)], axis=0)

    def step(start, carry, masked):
        m, l, acc = carry
        kb = k_ref[pl.ds(start, tk), :]
        vb = v_ref[pl.ds(start, tk), :]
        s = lax.dot_general(q2, kb, (((1,), (1,)), ((), ())), preferred_element_type=F32)
        if masked:
            r = lax.broadcasted_iota(jnp.int32, s.shape, 0)
            c = lax.broadcasted_iota(jnp.int32, s.shape, 1)
            qpos = jnp.where(r >= tq, r - tq, r)
            s = jnp.where(c <= qpos, s, -jnp.inf)
        m_new = jnp.maximum(m, jnp.max(s, axis=-1, keepdims=True))
        alpha = jnp.exp(m - m_new)
        p = jnp.exp(s - m_new)
        l = alpha * l + jnp.sum(p, axis=-1, keepdims=True)
        acc = alpha * acc + jnp.dot(p.astype(BF16), vb, preferred_element_type=F32)
        return m_new, l, acc

    init = (jnp.full((2 * tq, 1), -jnp.inf, F32), jnp.zeros((2 * tq, 1), F32),
            jnp.zeros((2 * tq, LANES), F32))
    n_full = (qi * tq) // tk
    carry = lax.fori_loop(
        0, n_full, lambda j, cr: step(pl.multiple_of(j * tk, tk), cr, False), init)
    m, l, acc = step(pl.multiple_of(qi * tq, tq), carry, True)

    lp = lam_ref[...]
    lam = (jnp.exp(jnp.sum(lp[0:1, :] * lp[1:2, :], axis=-1, keepdims=True))
           - jnp.exp(jnp.sum(lp[2:3, :] * lp[3:4, :], axis=-1, keepdims=True)) + lam_init)
    o = acc[:tq, :] / l[:tq, :] - lam * (acc[tq:, :] / l[tq:, :])
    o = o * lax.rsqrt(jnp.mean(o * o, axis=-1, keepdims=True) + LN_EPS) * g_ref[...]
    o_ref[...] = (o * (1.0 - lam_init)).astype(BF16)


def _diff_attention(l, q3, k3, v3, lam_p, subln_g, lam_init, tq):
    bsz, seq, bw = q3.shape
    heads = bw // LANES
    kern = functools.partial(_attn_kernel, tq=tq, tk=tq, lam_init=lam_init)
    return pl.pallas_call(
        kern,
        grid=(bsz, heads, seq // tq),
        in_specs=[
            pl.BlockSpec((None, tq, LANES), lambda b, h, i: (b, i, h)),
            pl.BlockSpec((None, seq, LANES), lambda b, h, i: (b, 0, h)),
            pl.BlockSpec((None, seq, LANES), lambda b, h, i: (b, 0, h)),
            pl.BlockSpec((None, 4, lam_p.shape[-1]), lambda b, h, i: (l, 0, 0)),
            pl.BlockSpec((None, 1, LANES), lambda b, h, i: (l, 0, 0)),
        ],
        out_specs=pl.BlockSpec((None, tq, LANES), lambda b, h, i: (b, i, h)),
        out_shape=jax.ShapeDtypeStruct((bsz, seq, bw), BF16),
        compiler_params=_cparams("parallel", "parallel", "arbitrary"),
        name="diff_attn",
    )(q3, k3, v3, lam_p, subln_g)


def _outproj_kernel(ya_ref, yb_ref, yc_ref, x_ref, mod_ref, w_ref, b_ref, g_ref, beta_ref, o_ref,
                    *, alpha):
    a_w = ya_ref.shape[1]
    b_w = yb_ref.shape[1]
    y = (jnp.dot(ya_ref[...], w_ref[0:a_w, :], preferred_element_type=F32)
         + jnp.dot(yb_ref[...], w_ref[a_w:a_w + b_w, :], preferred_element_type=F32)
         + jnp.dot(yc_ref[...], w_ref[a_w + b_w:, :], preferred_element_type=F32)
         + b_ref[...])
    r = alpha * x_ref[...] + (1.0 + mod_ref[2:3, :]) * y
    o_ref[...] = _layer_norm(r, g_ref[...], beta_ref[...])


def _out_projection(l, ya, yb, yc, x2, mod, w_out, b_out, ln_g, ln_b, seq, tm, alpha):
    n, d = x2.shape
    row = lambda i: (i, 0)
    kern = functools.partial(_outproj_kernel, alpha=alpha)
    return pl.pallas_call(
        kern,
        grid=(n // tm,),
        in_specs=[
            pl.BlockSpec((tm, ya.shape[1]), row),
            pl.BlockSpec((tm, yb.shape[1]), row),
            pl.BlockSpec((tm, yc.shape[1]), row),
            pl.BlockSpec((tm, d), row),
            pl.BlockSpec((None, None, 6, d), lambda i: (l, (i * tm) // seq, 0, 0)),
            pl.BlockSpec((None,) + w_out.shape[1:], lambda i: (l, 0, 0)),
            pl.BlockSpec((None, 1, d), lambda i: (l, 0, 0)),
            pl.BlockSpec((None, None, 1, d), lambda i: (l, 0, 0, 0)),
            pl.BlockSpec((None, None, 1, d), lambda i: (l, 0, 0, 0)),
        ],
        out_specs=pl.BlockSpec((tm, d), row),
        out_shape=jax.ShapeDtypeStruct((n, d), F32),
        compiler_params=_cparams("parallel"),
        name="out_proj_ln",
    )(ya, yb, yc, x2, mod, w_out, b_out, ln_g, ln_b)


def _ffn_kernel(x_ref, xh_ref, mod_ref, wg_ref, wv_ref, bg_ref, bv_ref, cwg_ref, cwv_ref,
                cbg_ref, cbv_ref, wd_ref, bd_ref, g_ref, beta_ref, o_ref,
                hb_ref, hid_ref, *, alpha, seq, n_chunks, tf):
    tm = x_ref.shape[0]
    halo = xh_ref.shape[0]
    sc = 1.0 + mod_ref[4:5, :]
    sh = mod_ref[3:4, :]
    x = x_ref[...]
    hb_ref[halo:, :] = (x * sc + sh).astype(BF16)
    hb_ref[0:halo, :] = (xh_ref[...] * sc + sh).astype(BF16)
    keep = jnp.where((pl.program_id(0) * tm) % seq == 0, 0.0, 1.0).astype(F32)

    def conv3(z, bias_ref, cw_ref, cb_ref, c):
        z = z + bias_ref[c]
        z = jnp.concatenate([z[0:halo, :] * keep, z[halo:, :]], axis=0)
        cw = cw_ref[c]
        out = cb_ref[c] + z[halo:, :] * cw[FFN_KERNEL - 1:FFN_KERNEL, :]
        for back in range(1, FFN_KERNEL):
            k = FFN_KERNEL - 1 - back
            out = out + pltpu.roll(z, back, 0)[halo:, :] * cw[k:k + 1, :]
        return out

    hb = hb_ref[...]
    for c in range(n_chunks):
        gate = conv3(jnp.dot(hb, wg_ref[c], preferred_element_type=F32), bg_ref, cwg_ref, cbg_ref, c)
        gate = gate * jax.nn.sigmoid(gate)
        val = conv3(jnp.dot(hb, wv_ref[c], preferred_element_type=F32), bv_ref, cwv_ref, cbv_ref, c)
        hid_ref[:, c * tf:(c + 1) * tf] = (gate * val).astype(BF16)

    y = jnp.dot(hid_ref[...], wd_ref[...], preferred_element_type=F32) + bd_ref[...]
    r = alpha * x + (1.0 + mod_ref[5:6, :]) * y
    o_ref[...] = _layer_norm(r, g_ref[...], beta_ref[...])


def _conv_ffn(l, x2, mod, wg, wv, bg, bv, cwg, cwv, cbg, cbv, w_down, b_down, ln_g, ln_b,
              seq, tm, alpha):
    n, d = x2.shape
    n_chunks, _, tf = wg.shape[1:]
    f = n_chunks * tf
    halo = SUBLANES
    row = lambda i: (i, 0)
    lay4 = lambda i: (l, 0, 0, 0)
    lay3 = lambda i: (l, 0, 0)
    kern = functools.partial(_ffn_kernel, alpha=alpha, seq=seq, n_chunks=n_chunks, tf=tf)
    return pl.pallas_call(
        kern,
        grid=(n // tm,),
        in_specs=[
            pl.BlockSpec((tm, d), row),
            pl.BlockSpec((halo, d), lambda i: (jnp.maximum(i * (tm // halo) - 1, 0), 0)),
            pl.BlockSpec((None, None, 6, d), lambda i: (l, (i * tm) // seq, 0, 0)),
            pl.BlockSpec((None, n_chunks, d, tf), lay4),
            pl.BlockSpec((None, n_chunks, d, tf), lay4),
            pl.BlockSpec((None, n_chunks, 1, tf), lay4),
            pl.BlockSpec((None, n_chunks, 1, tf), lay4),
            pl.BlockSpec((None, n_chunks, FFN_KERNEL, tf), lay4),
            pl.BlockSpec((None, n_chunks, FFN_KERNEL, tf), lay4),
            pl.BlockSpec((None, n_chunks, 1, tf), lay4),
            pl.BlockSpec((None, n_chunks, 1, tf), lay4),
            pl.BlockSpec((None, f, d), lay3),
            pl.BlockSpec((None, 1, d), lay3),
            pl.BlockSpec((None, None, 1, d), lambda i: (l, 1, 0, 0)),
            pl.BlockSpec((None, None, 1, d), lambda i: (l, 1, 0, 0)),
        ],
        out_specs=pl.BlockSpec((tm, d), row),
        out_shape=jax.ShapeDtypeStruct((n, d), F32),
        scratch_shapes=[
            pltpu.VMEM((tm + halo, d), BF16),
            pltpu.VMEM((tm, f), BF16),
        ],
        compiler_params=_cparams("parallel"),
        name="conv_ffn_ln",
    )(x2, x2, mod, wg, wv, bg, bv, cwg, cwv, cbg, cbv, w_down, b_down, ln_g, ln_b)


def _chunk_cols(w, n_chunks, tf):
    depth, r, _ = w.shape
    return w.reshape(depth, r, n_chunks, tf).transpose(0, 2, 1, 3)


def kernel(x, c, positions, w_ada, b_ada, w_in, b_in, conv_a_w, conv_a_b, gn_a_g, gn_a_b, lam_p, subln_g, ln_c_g, ln_c_b, w_sp, b_sp, w_out, b_out, w_up, b_up, conv_f_w, conv_f_b, w_down, b_down, ln_g, ln_b):
    bsz, seq, d = x.shape
    depth = w_ada.shape[0]
    n = bsz * seq
    a_w = conv_a_w.shape[-1]
    c_w = ln_c_g.shape[-1]
    head_dim = lam_p.shape[-1]
    f = w_down.shape[1]
    alpha = (2 * depth) ** 0.25
    tm = min(512, seq)
    tf = 256
    n_chunks = f // tf

    mod = _modulation(c, w_ada, b_ada).reshape(depth, bsz, 6, d)
    cos_t, sin_t = _rope_tables(positions, head_dim)
    w_in_b = w_in.astype(BF16)
    b_in3 = b_in.reshape(depth, 1, -1)
    wsp = w_sp.transpose(0, 2, 1, 3).reshape(depth, CHUNK, C_HEADS * CHUNK)
    bsp = jnp.repeat(b_sp.transpose(0, 2, 1), c_w // C_HEADS, axis=-1)
    gidx = jnp.arange(a_w) // (a_w // A_GROUPS)
    gmat = jnp.where(gidx[:, None] == gidx[None, :], 1.0 / (a_w // A_GROUPS), 0.0).astype(BF16)
    w_out_b = w_out.astype(BF16)
    w_up_b = w_up.astype(BF16)
    wg = _chunk_cols(w_up_b[:, :, :f], n_chunks, tf)
    wv = _chunk_cols(w_up_b[:, :, f:], n_chunks, tf)
    bg = _chunk_cols(b_up[:, None, :f], n_chunks, tf)
    bv = _chunk_cols(b_up[:, None, f:], n_chunks, tf)
    cwg = _chunk_cols(conv_f_w[:, :, :f], n_chunks, tf)
    cwv = _chunk_cols(conv_f_w[:, :, f:], n_chunks, tf)
    cbg = _chunk_cols(conv_f_b[:, None, :f], n_chunks, tf)
    cbv = _chunk_cols(conv_f_b[:, None, f:], n_chunks, tf)
    w_down_b = w_down.astype(BF16)
    vec3 = lambda a: a.reshape(depth, 1, -1)
    ln_g4 = ln_g.reshape(depth, 2, 1, d)
    ln_b4 = ln_b.reshape(depth, 2, 1, d)

    x2 = x.reshape(n, d)
    for l in range(depth):
        lam_init = 0.8 - 0.6 * math.exp(-0.3 * l)
        yg, q, k, v, yc = _in_projection(l, x2, mod, cos_t, sin_t, w_in_b, b_in3,
                                         vec3(ln_c_g), vec3(ln_c_b), wsp, bsp, a_w, seq, tm)
        ya = _conformer_conv(l, yg.reshape(bsz, seq, a_w), conv_a_w, vec3(conv_a_b), gmat,
                             vec3(gn_a_g), vec3(gn_a_b), tm)
        bw = q.shape[-1]
        yb = _diff_attention(l, q.reshape(bsz, seq, bw), k.reshape(bsz, seq, bw),
                             v.reshape(bsz, seq, bw), lam_p, vec3(subln_g), lam_init, tm)
        x2 = _out_projection(l, ya.reshape(n, a_w), yb.reshape(n, bw), yc, x2, mod, w_out_b,
                             vec3(b_out), ln_g4, ln_b4, seq, tm, alpha)
        x2 = _conv_ffn(l, x2, mod, wg, wv, bg, bv, cwg, cwv, cbg, cbv, w_down_b, vec3(b_down),
                       ln_g4, ln_b4, seq, tm, alpha)
    return x2.reshape(bsz, seq, d)
```
